```python
import math, functools
import jax, jax.numpy as jnp
from jax import lax
import numpy as np

D_MODEL = 2048
BATCH = 4
SEQ = 8192
DEPTH = 2
DEC_BATCH = 32
DEC_SEQ = 16
PAST_LEN = 1024

CHUNK = 64
WINDOW = 128
WIN_CHUNKS = WINDOW // CHUNK
BAND = (WIN_CHUNKS + 1) * CHUNK
SWA_KEEP = min(WINDOW, PAST_LEN)
HEAD_DIM = 64
Q_HEADS = 16
KV_HEADS = 4
Q_PER_KV = Q_HEADS // KV_HEADS
ATTN_WIDTH = Q_HEADS * HEAD_DIM
KV_WIDTH = KV_HEADS * HEAD_DIM
ATTN_SCALE = HEAD_DIM ** -0.5
ROPE_THETA = 10000.0
SSM_HEADS = 16
SSM_HEAD_DIM = 64
SSM_INNER = SSM_HEADS * SSM_HEAD_DIM
SSM_GROUPS = 4
SSM_HEADS_PER_GROUP = SSM_HEADS // SSM_GROUPS
SSM_STATE = 128
SSM_GN = SSM_GROUPS * SSM_STATE
SSM_CONV = 4
SSM_CONV_DIM = SSM_INNER + 2 * SSM_GN
SSD_CHUNK = 128
Q_END = ATTN_WIDTH
K_END = Q_END + KV_WIDTH
V_END = K_END + KV_WIDTH
Z_END = V_END + SSM_INNER
XBC_END = Z_END + SSM_CONV_DIM
IN_WIDTH = XBC_END + SSM_HEADS
MIX_WIDTH = ATTN_WIDTH + SSM_INNER
CONF_DIM = D_MODEL
CONF_KERNEL = 31
D_FF = 4 * D_MODEL
N_EVEN = (DEPTH + 1) // 2
N_ODD = DEPTH // 2
EPS = 1e-6
F32 = jnp.float32

kernel_name = 'hybrid_stream_swa_ssd_conformer'


def rms_norm(x, g):
    xf = x.astype(F32)
    y = xf * lax.rsqrt(jnp.mean(xf * xf, axis=-1, keepdims=True) + EPS)
    return (y * g.astype(F32)).astype(x.dtype)


def layer_norm(x, g, b):
    xf = x.astype(F32)
    mu = jnp.mean(xf, axis=-1, keepdims=True)
    xc = xf - mu
    var = jnp.mean(xc * xc, axis=-1, keepdims=True)
    return (xc * lax.rsqrt(var + EPS) * g.astype(F32) + b.astype(F32)).astype(x.dtype)


def adaln(c, w, b):
    m = (jax.nn.silu(c) @ w + b)[:, None, :]
    return jnp.split(m, 6, axis=-1)


def rope(x, pos):
    half = HEAD_DIM // 2
    inv = ROPE_THETA ** (-jnp.arange(half, dtype=F32) / half)
    ang = pos.astype(F32)[:, None] * inv[None, :]
    shp = (1, x.shape[1]) + (1,) * (x.ndim - 3) + (half,)
    cos = jnp.cos(ang).reshape(shp)
    sin = jnp.sin(ang).reshape(shp)
    xf = x.astype(F32)
    x1, x2 = xf[..., :half], xf[..., half:]
    return jnp.concatenate([x1 * cos - x2 * sin, x2 * cos + x1 * sin], axis=-1).astype(x.dtype)


def sink_probs(s, sinks):
    sk = sinks.astype(F32)[:, :, None]
    m = jnp.maximum(jnp.max(s, axis=-1), sk)
    p = jnp.exp(s - m[..., None])
    return p / (jnp.sum(p, axis=-1) + jnp.exp(sk - m))[..., None]


def swa_prompt(q, k, v, sinks):
    bsz, T = q.shape[0], q.shape[1]
    nb = T // CHUNK
    pad = WIN_CHUNKS * CHUNK

    def band(t):
        tp = jnp.pad(t, ((0, 0), (pad, 0), (0, 0), (0, 0))).reshape(bsz, nb + WIN_CHUNKS, CHUNK, KV_HEADS, HEAD_DIM)
        return jnp.concatenate([tp[:, j:j + nb] for j in range(WIN_CHUNKS + 1)], axis=2)

    kb, vb = band(k), band(v)
    qb = q.reshape(bsz, nb, CHUNK, KV_HEADS, Q_PER_KV, HEAD_DIM)
    s = jnp.einsum('bnqkgd,bnskd->bnkgqs', qb, kb, preferred_element_type=F32) * ATTN_SCALE
    key_pos = (jnp.arange(nb) * CHUNK)[:, None] + jnp.arange(BAND)[None, :] - pad
    s = jnp.where((key_pos >= 0)[None, :, None, None, None, :], s, -jnp.inf)
    pr = sink_probs(s, sinks).astype(v.dtype)
    o = jnp.einsum('bnkgqs,bnskd->bnqkgd', pr, vb)
    return o.reshape(bsz, T, KV_HEADS, Q_PER_KV, HEAD_DIM)


def swa_sample(q, k, v, sinks, cache_k, cache_v):
    kall = jnp.concatenate([cache_k.astype(k.dtype), k], axis=1)
    vall = jnp.concatenate([cache_v.astype(v.dtype), v], axis=1)
    s = jnp.einsum('bqkgd,bskd->bkgqs', q, kall, preferred_element_type=F32) * ATTN_SCALE
    pr = sink_probs(s, sinks).astype(v.dtype)
    return jnp.einsum('bkgqs,bskd->bqkgd', pr, vall)


def causal_dwconv(x, ctx, w, b):
    xp = jnp.concatenate([ctx.astype(x.dtype), x], axis=1)
    y = lax.conv_general_dilated(xp, w.astype(x.dtype)[:, None, :], window_strides=(1,), padding='VALID',
                                 dimension_numbers=('NWC', 'WIO', 'NWC'), feature_group_count=x.shape[-1])
    return y + b.astype(x.dtype), xp[:, xp.shape[1] - (w.shape[0] - 1):]


def ssd_scan(x, dt, a, bm, cm, h0, chunk):
    bsz, T, G, R, P = x.shape
    N = bm.shape[-1]
    nc = T // chunk
    xc = x.astype(F32).reshape(bsz, nc, chunk, G, R, P)
    bc = bm.astype(F32).reshape(bsz, nc, chunk, G, N)
    cc = cm.astype(F32).reshape(bsz, nc, chunk, G, N)
    dtc = dt.reshape(bsz, nc, chunk, G, R)
    acum = jnp.cumsum(dtc * a, axis=2)
    causal = jnp.tril(jnp.ones((chunk, chunk), bool))[:, :, None, None]
    seg = acum[:, :, :, None] - acum[:, :, None, :]
    decay = jnp.exp(jnp.where(causal, seg, -jnp.inf))
    cb = jnp.einsum('bclgn,bcsgn->bclsg', cc, bc)
    xdt = xc * dtc[..., None]
    y_diag = jnp.einsum('bclsg,bclsgr,bcsgrp->bclgrp', cb, decay, xdt)
    to_end = jnp.exp(acum[:, :, -1:] - acum)
    states = jnp.einsum('bclgn,bclgrp->bcgrpn', bc, xdt * to_end[..., None])
    chunk_decay = jnp.exp(acum[:, :, -1])

    def step(h, inp):
        st, dec = inp
        return h * dec[..., None, None] + st, h

    h_t, h_in = lax.scan(step, h0.astype(F32), (jnp.moveaxis(states, 1, 0), jnp.moveaxis(chunk_decay, 1, 0)))
    h_in = jnp.moveaxis(h_in, 0, 1)
    y_off = jnp.einsum('bclgn,bcgrpn->bclgrp', cc, h_in) * jnp.exp(acum)[..., None]
    return (y_diag + y_off).reshape(bsz, T, G, R, P), h_t


def gated_group_rms(y, z, g):
    bsz, T, _ = y.shape
    u = (y.astype(F32) * jax.nn.silu(z.astype(F32))).reshape(bsz, T, SSM_GROUPS, SSM_INNER // SSM_GROUPS)
    u = u * lax.rsqrt(jnp.mean(u * u, axis=-1, keepdims=True) + EPS)
    return u.reshape(bsz, T, SSM_INNER) * g.astype(F32)


def attn_ssm_mixer(h, pos, attend, conv_ctx, h0, ssd_chunk, w_in, w_out, sinks, a_log, dt_bias, d_skip, conv_w, conv_b, norm_g):
    bsz, T, _ = h.shape
    proj = h @ w_in
    q = rope(proj[..., :Q_END].reshape(bsz, T, KV_HEADS, Q_PER_KV, HEAD_DIM), pos)
    k = rope(proj[..., Q_END:K_END].reshape(bsz, T, KV_HEADS, HEAD_DIM), pos)
    v = proj[..., K_END:V_END].reshape(bsz, T, KV_HEADS, HEAD_DIM)
    z = proj[..., V_END:Z_END]
    xbc = proj[..., Z_END:XBC_END]
    dt = proj[..., XBC_END:]
    o_attn = attend(q, k, v, sinks.reshape(KV_HEADS, Q_PER_KV)).reshape(bsz, T, ATTN_WIDTH)
    xbc, conv_state = causal_dwconv(xbc, conv_ctx, conv_w, conv_b)
    xbc = jax.nn.silu(xbc)
    xs = xbc[..., :SSM_INNER].reshape(bsz, T, SSM_GROUPS, SSM_HEADS_PER_GROUP, SSM_HEAD_DIM)
    bm = xbc[..., SSM_INNER:SSM_INNER + SSM_GN].reshape(bsz, T, SSM_GROUPS, SSM_STATE)
    cm = xbc[..., SSM_INNER + SSM_GN:].reshape(bsz, T, SSM_GROUPS, SSM_STATE)
    dt = jax.nn.softplus(dt.astype(F32) + dt_bias.astype(F32)).reshape(bsz, T, SSM_GROUPS, SSM_HEADS_PER_GROUP)
    a = -jnp.exp(a_log.astype(F32)).reshape(SSM_GROUPS, SSM_HEADS_PER_GROUP)
    y, h_t = ssd_scan(xs, dt, a, bm, cm, h0, ssd_chunk)
    y = y + d_skip.astype(F32).reshape(SSM_GROUPS, SSM_HEADS_PER_GROUP)[:, :, None] * xs.astype(F32)
    y = gated_group_rms(y.reshape(bsz, T, SSM_INNER), z, norm_g).astype(h.dtype)
    out = jnp.concatenate([o_attn, y], axis=-1) @ w_out
    return out, k, v, h_t.reshape(bsz, SSM_HEADS, SSM_HEAD_DIM, SSM_STATE), conv_state


def conformer_conv(h, ctx, w1, b1, dw_w, dw_b, ln_g, ln_b, w2, b2):
    u = h @ w1 + b1
    u = u[..., :CONF_DIM] * jax.nn.sigmoid(u[..., CONF_DIM:])
    u, new_ctx = causal_dwconv(u, ctx, dw_w, dw_b)
    u = jax.nn.silu(layer_norm(u, ln_g, ln_b))
    return u @ w2 + b2, new_ctx


def sq_relu_mlp(h, w_up, w_down):
    return jnp.square(jax.nn.relu(h @ w_up)) @ w_down


def trunk(x, c, pos, p, attend, ssm_ctx, ssm_h0, conf_ctx, ssd_chunk):
    ks, vs, hs, scs, ccs = [], [], [], [], []
    for i in range(DEPTH):
        sh1, sc1, gt1, sh2, sc2, gt2 = adaln(c, p['w_mod'][i], p['b_mod'][i])
        h = rms_norm(x, p['g_mix'][i]) * (1.0 + sc1) + sh1
        if i % 2 == 0:
            e = i // 2
            out, k, v, h_t, sc = attn_ssm_mixer(
                h, pos, functools.partial(attend, e), ssm_ctx[e], ssm_h0[e], ssd_chunk,
                p['w_in'][e], p['w_out'][e], p['attn_sinks'][e], p['ssm_a_log'][e], p['ssm_dt_bias'][e],
                p['ssm_d'][e], p['ssm_conv_w'][e], p['ssm_conv_b'][e], p['ssm_norm_g'][e])
            ks.append(k); vs.append(v); hs.append(h_t); scs.append(sc)
        else:
            o = i // 2
            out, cc = conformer_conv(h, conf_ctx[o], p['conf_w1'][o], p['conf_b1'][o], p['conf_dw_w'][o],
                                     p['conf_dw_b'][o], p['conf_ln_g'][o], p['conf_ln_b'][o],
                                     p['conf_w2'][o], p['conf_b2'][o])
            ccs.append(cc)
        x = x + gt1 * out
        h = rms_norm(x, p['g_mlp'][i]) * (1.0 + sc2) + sh2
        x = x + gt2 * sq_relu_mlp(h, p['mlp_w_up'][i], p['mlp_w_down'][i])
    return rms_norm(x, p['g_final']), ks, vs, hs, scs, ccs


def setup_inputs(seed: int = 0) -> dict:
    key = jax.random.key(seed)
    keys = iter(jax.random.split(key, 48))

    def nrm(shape, scale=1.0):
        return jax.random.normal(next(keys), shape, F32) * scale

    dt0 = jnp.exp(jax.random.uniform(next(keys), (N_EVEN, SSM_HEADS), F32) * (math.log(0.1) - math.log(0.001)) + math.log(0.001))
    return {
        'x_prompt': nrm((BATCH, SEQ, D_MODEL)),
        'x_sample': nrm((DEC_BATCH, DEC_SEQ, D_MODEL)),
        'cache_swa_k': nrm((N_EVEN, DEC_BATCH, SWA_KEEP, KV_HEADS, HEAD_DIM)),
        'cache_swa_v': nrm((N_EVEN, DEC_BATCH, SWA_KEEP, KV_HEADS, HEAD_DIM)),
        'state_ssm': nrm((N_EVEN, DEC_BATCH, SSM_HEADS, SSM_HEAD_DIM, SSM_STATE), 0.1),
        'state_ssm_conv': nrm((N_EVEN, DEC_BATCH, SSM_CONV - 1, SSM_CONV_DIM)),
        'state_conf_conv': nrm((N_ODD, DEC_BATCH, CONF_KERNEL - 1, CONF_DIM), 0.5),
        'c_prompt': nrm((BATCH, D_MODEL)),
        'c_sample': nrm((DEC_BATCH, D_MODEL)),
        'w_mod': nrm((DEPTH, D_MODEL, 6 * D_MODEL), 0.5 * D_MODEL ** -0.5),
        'b_mod': nrm((DEPTH, 6 * D_MODEL), 0.02),
        'g_mix': 1.0 + nrm((DEPTH, D_MODEL), 0.02),
        'g_mlp': 1.0 + nrm((DEPTH, D_MODEL), 0.02),
        'w_in': nrm((N_EVEN, D_MODEL, IN_WIDTH), D_MODEL ** -0.5),
        'w_out': nrm((N_EVEN, MIX_WIDTH, D_MODEL), MIX_WIDTH ** -0.5),
        'attn_sinks': nrm((N_EVEN, Q_HEADS), 0.5),
        'ssm_a_log': jnp.log(jax.random.uniform(next(keys), (N_EVEN, SSM_HEADS), F32, 1.0, 16.0)),
        'ssm_dt_bias': dt0 + jnp.log(-jnp.expm1(-dt0)),
        'ssm_d': 1.0 + nrm((N_EVEN, SSM_HEADS), 0.1),
        'ssm_conv_w': nrm((N_EVEN, SSM_CONV, SSM_CONV_DIM), SSM_CONV ** -0.5),
        'ssm_conv_b': nrm((N_EVEN, SSM_CONV_DIM), 0.02),
        'ssm_norm_g': 1.0 + nrm((N_EVEN, SSM_INNER), 0.02),
        'conf_w1': nrm((N_ODD, D_MODEL, 2 * CONF_DIM), D_MODEL ** -0.5),
        'conf_b1': nrm((N_ODD, 2 * CONF_DIM), 0.02),
        'conf_dw_w': nrm((N_ODD, CONF_KERNEL, CONF_DIM), CONF_KERNEL ** -0.5),
        'conf_dw_b': nrm((N_ODD, CONF_DIM), 0.02),
        'conf_ln_g': 1.0 + nrm((N_ODD, CONF_DIM), 0.02),
        'conf_ln_b': nrm((N_ODD, CONF_DIM), 0.02),
        'conf_w2': nrm((N_ODD, CONF_DIM, D_MODEL), CONF_DIM ** -0.5),
        'conf_b2': nrm((N_ODD, D_MODEL), 0.02),
        'mlp_w_up': nrm((DEPTH, D_MODEL, D_FF), D_MODEL ** -0.5),
        'mlp_w_down': nrm((DEPTH, D_FF, D_MODEL), D_FF ** -0.5),
        'g_final': 1.0 + nrm((D_MODEL,), 0.02),
    }


def reference(x_prompt, x_sample, cache_swa_k, cache_swa_v, state_ssm, state_ssm_conv, state_conf_conv,
              c_prompt, c_sample, w_mod, b_mod, g_mix, g_mlp, w_in, w_out, attn_sinks, ssm_a_log, ssm_dt_bias,
              ssm_d, ssm_conv_w, ssm_conv_b, ssm_norm_g, conf_w1, conf_b1, conf_dw_w, conf_dw_b, conf_ln_g,
              conf_ln_b, conf_w2, conf_b2, mlp_w_up, mlp_w_down, g_final):
    p = dict(w_mod=w_mod, b_mod=b_mod, g_mix=g_mix, g_mlp=g_mlp, w_in=w_in, w_out=w_out, attn_sinks=attn_sinks,
             ssm_a_log=ssm_a_log, ssm_dt_bias=ssm_dt_bias, ssm_d=ssm_d, ssm_conv_w=ssm_conv_w,
             ssm_conv_b=ssm_conv_b, ssm_norm_g=ssm_norm_g, conf_w1=conf_w1, conf_b1=conf_b1,
             conf_dw_w=conf_dw_w, conf_dw_b=conf_dw_b, conf_ln_g=conf_ln_g, conf_ln_b=conf_ln_b,
             conf_w2=conf_w2, conf_b2=conf_b2, mlp_w_up=mlp_w_up, mlp_w_down=mlp_w_down, g_final=g_final)

    bp, tp = x_prompt.shape[0], x_prompt.shape[1]
    zero_sc = [jnp.zeros((bp, SSM_CONV - 1, SSM_CONV_DIM), x_prompt.dtype) for _ in range(N_EVEN)]
    zero_h = [jnp.zeros((bp, SSM_GROUPS, SSM_HEADS_PER_GROUP, SSM_HEAD_DIM, SSM_STATE), F32) for _ in range(N_EVEN)]
    zero_cc = [jnp.zeros((bp, CONF_KERNEL - 1, CONF_DIM), x_prompt.dtype) for _ in range(N_ODD)]
    y_prompt, kp, vp, hp, scp, ccp = trunk(
        x_prompt, c_prompt, jnp.arange(tp), p, lambda e, q, k, v, s: swa_prompt(q, k, v, s),
        zero_sc, zero_h, zero_cc, min(SSD_CHUNK, tp))

    bs, ts = x_sample.shape[0], x_sample.shape[1]
    h0_s = [state_ssm[e].reshape(bs, SSM_GROUPS, SSM_HEADS_PER_GROUP, SSM_HEAD_DIM, SSM_STATE).astype(F32)
            for e in range(N_EVEN)]
    y_sample, ksm, vsm, hsm, scsm, ccsm = trunk(
        x_sample, c_sample, PAST_LEN + jnp.arange(ts), p,
        lambda e, q, k, v, s: swa_sample(q, k, v, s, cache_swa_k[e], cache_swa_v[e]),
        [state_ssm_conv[e] for e in range(N_EVEN)], h0_s, [state_conf_conv[o] for o in range(N_ODD)], ts)

    swa_k_prompt = jnp.stack([k[:, k.shape[1] - WINDOW:] for k in kp])
    swa_v_prompt = jnp.stack([v[:, v.shape[1] - WINDOW:] for v in vp])
    swa_k_sample = jnp.stack(ksm)
    swa_v_sample = jnp.stack(vsm)
    ssm_state_prompt = jnp.stack(hp)
    ssm_state_sample = jnp.stack(hsm)
    ssm_conv_prompt = jnp.stack(scp)
    ssm_conv_sample = jnp.stack(scsm)
    conf_conv_prompt = jnp.stack(ccp)
    conf_conv_sample = jnp.stack(ccsm)
    return (y_prompt, y_sample, swa_k_prompt, swa_v_prompt, swa_k_sample, swa_v_sample,
            ssm_state_prompt, ssm_state_sample, ssm_conv_prompt, ssm_conv_sample,
            conf_conv_prompt, conf_conv_sample)
```

```python
import functools
import math

import jax
import jax.numpy as jnp
from jax import lax
from jax.experimental import pallas as pl
from jax.experimental.pallas import tpu as pltpu

F32 = jnp.float32
BF16 = jnp.bfloat16

D_MODEL = 2048
CHUNK = 64
WINDOW = 128
PAST_LEN = 1024
HEAD_DIM = 64
Q_HEADS = 16
KV_HEADS = 4
Q_PER_KV = Q_HEADS // KV_HEADS
ATTN_WIDTH = Q_HEADS * HEAD_DIM
KV_WIDTH = KV_HEADS * HEAD_DIM
ATTN_SCALE = HEAD_DIM ** -0.5
ROPE_THETA = 10000.0
SSM_HEADS = 16
SSM_HEAD_DIM = 64
SSM_INNER = SSM_HEADS * SSM_HEAD_DIM
SSM_GROUPS = 4
SSM_STATE = 128
SSM_GN = SSM_GROUPS * SSM_STATE
SSM_CONV = 4
SSM_CONV_DIM = SSM_INNER + 2 * SSM_GN
SSD_CHUNK = 128
Q_END = ATTN_WIDTH
K_END = Q_END + KV_WIDTH
V_END = K_END + KV_WIDTH
Z_END = V_END + SSM_INNER
XBC_END = Z_END + SSM_CONV_DIM
IN_WIDTH = XBC_END + SSM_HEADS
CONF_KERNEL = 31
D_FF = 4 * D_MODEL
EPS = 1e-6

LANES = 128
SUBLANES = 8
QX_WIDTH = Q_HEADS * LANES
IN_PAD = XBC_END + LANES
CONF_HALO = 32
MIB = 1024 * 1024


def _cparams(sem, vmem_mib):
    return pltpu.CompilerParams(dimension_semantics=sem, vmem_limit_bytes=vmem_mib * MIB)


def _resident(shape):
    nd = len(shape)
    return pl.BlockSpec(shape, lambda *_: (0,) * nd, pipeline_mode=pl.Buffered(1))


def _silu(x):
    return x * jax.nn.sigmoid(x)


def _norm_mod(x, g, sc, sh):
    y = x * lax.rsqrt(jnp.mean(x * x, axis=-1, keepdims=True) + EPS) * g
    return y * (1.0 + sc) + sh


def _nt(a, b):
    return lax.dot_general(a, b, (((1,), (1,)), ((), ())), preferred_element_type=F32)


def _dot(a, b):
    return jnp.dot(a, b, preferred_element_type=F32)


def _split2(v):
    hi = v.astype(BF16)
    mid = (v - hi.astype(F32)).astype(BF16)
    return hi, mid


def _mod_kernel(c_ref, w_ref, b_ref, o_ref):
    a = _silu(c_ref[...]).astype(BF16)
    o_ref[0] = _dot(a, w_ref[0].astype(BF16)) + b_ref[0]


def _modulation(c_all, w_mod, b_mod):
    depth, d, n = w_mod.shape
    rows = c_all.shape[0]
    tn = 1024
    return pl.pallas_call(
        _mod_kernel,
        grid=(depth, n // tn),
        in_specs=[
            pl.BlockSpec((rows, d), lambda l, j: (0, 0)),
            pl.BlockSpec((1, d, tn), lambda l, j: (l, 0, j)),
            pl.BlockSpec((1, 1, tn), lambda l, j: (l, 0, j)),
        ],
        out_specs=pl.BlockSpec((1, rows, tn), lambda l, j: (l, 0, j)),
        out_shape=jax.ShapeDtypeStruct((depth, rows, n), F32),
        compiler_params=_cparams(("arbitrary", "arbitrary"), 40),
        name="adaln_mod",
    )(c_all, w_mod, b_mod.reshape(depth, 1, n))


def _mod_spec(mod, k, blocks_per_group):
    return pl.BlockSpec((None, mod.shape[1], D_MODEL), lambda i, *_: (i // blocks_per_group, 0, k))


def _inproj_kernel(x_ref, g_ref, sh_ref, sc_ref, cos_ref, sa_ref, sb_ref, w_ref,
                   q_ref, k_ref, v_ref, z_ref, xbc_ref, dt_ref):
    a = _norm_mod(x_ref[...], g_ref[...], sc_ref[...], sh_ref[...]).astype(BF16)
    cos, sa, sb = cos_ref[...], sa_ref[...], sb_ref[...]
    lane = lax.broadcasted_iota(jnp.int32, (a.shape[0], LANES), 1)
    low = lane < HEAD_DIM

    def rope(t):
        return t * cos + pltpu.roll(t, LANES - HEAD_DIM // 2, 1) * sa + pltpu.roll(t, HEAD_DIM // 2, 1) * sb

    q = _dot(a, w_ref[:, :Q_END])
    for j in range(Q_END // LANES):
        t = rope(q[:, j * LANES:(j + 1) * LANES]) * ATTN_SCALE
        kv_half = (j // 2) % 2
        for half in range(2):
            th = jnp.where(low if half == 0 else ~low, t, 0.0)
            if half != kv_half:
                th = pltpu.roll(th, HEAD_DIM, 1)
            hq = 2 * j + half
            q_ref[:, hq * LANES:(hq + 1) * LANES] = th.astype(BF16)
    k = _dot(a, w_ref[:, Q_END:K_END])
    for j in range(KV_WIDTH // LANES):
        k_ref[:, j * LANES:(j + 1) * LANES] = rope(k[:, j * LANES:(j + 1) * LANES])
    v_ref[...] = _dot(a, w_ref[:, K_END:V_END])
    z_ref[...] = _dot(a, w_ref[:, V_END:Z_END])
    xbc_ref[...] = _dot(a, w_ref[:, Z_END:XBC_END])
    dt_ref[...] = _dot(a, w_ref[:, XBC_END:IN_PAD])


def _inproj(x, g, mod, blocks_per_group, tabs, tab_blocks, w, tm):
    m = x.shape[0]
    row = lambda n: pl.BlockSpec((tm, n), lambda i: (i, 0))
    tab = pl.BlockSpec((tm, LANES), lambda i: (i % tab_blocks, 0))
    outs = [(QX_WIDTH, BF16), (KV_WIDTH, F32), (KV_WIDTH, F32), (SSM_INNER, F32), (SSM_CONV_DIM, F32), (LANES, F32)]
    return pl.pallas_call(
        _inproj_kernel,
        grid=(m // tm,),
        in_specs=[row(D_MODEL), _resident((1, D_MODEL)), _mod_spec(mod, 0, blocks_per_group),
                  _mod_spec(mod, 1, blocks_per_group), tab, tab, tab, _resident(w.shape)],
        out_specs=[row(n) for n, _ in outs],
        out_shape=[jax.ShapeDtypeStruct((m, n), dt) for n, dt in outs],
        compiler_params=_cparams(("arbitrary",), 56),
        name="in_proj",
    )(x, g, mod, mod, *tabs, w)


def _attend(qm, kg, vg, sk, valid):
    s = _nt(qm, kg)
    if valid is not None:
        s = jnp.where(valid, s, -jnp.inf)
    m = jnp.maximum(jnp.max(s, axis=1, keepdims=True), sk)
    p = jnp.exp(s - m)
    den = jnp.sum(p, axis=1, keepdims=True) + jnp.exp(sk - m)
    return _dot(p.astype(BF16), vg) / den


def _sink_column(sink_ref, h, rows):
    grp = lax.broadcasted_iota(jnp.int32, (Q_PER_KV * rows, 1), 0) // rows
    sk = jnp.full((Q_PER_KV * rows, 1), sink_ref[h * Q_PER_KV], F32)
    for g in range(1, Q_PER_KV):
        sk = jnp.where(grp == g, sink_ref[h * Q_PER_KV + g], sk)
    return sk


def _store_heads(o, o_ref, rows, r0, h, low):
    for jj in range(2):
        o0 = o[(2 * jj) * rows:(2 * jj + 1) * rows]
        o1 = o[(2 * jj + 1) * rows:(2 * jj + 2) * rows]
        if h % 2 == 1:
            o0 = pltpu.roll(o0, HEAD_DIM, 1)
        else:
            o1 = pltpu.roll(o1, HEAD_DIM, 1)
        j = 2 * h + jj
        o_ref[pl.ds(r0, rows), j * LANES:(j + 1) * LANES] = jnp.where(low, o0, o1).astype(o_ref.dtype)


def _attn_prompt_kernel(sink_ref, q_ref, k_ref, kh_ref, v_ref, vh_ref, o_ref, kbuf, vbuf):
    i = pl.program_id(1)
    tq = q_ref.shape[0]
    pair = 2 * CHUNK
    kbuf[0:WINDOW, :] = kh_ref[...].astype(BF16)
    kbuf[WINDOW:, :] = k_ref[...].astype(BF16)
    vbuf[0:WINDOW, :] = vh_ref[...].astype(BF16)
    vbuf[WINDOW:, :] = v_ref[...].astype(BF16)
    rows = Q_PER_KV * pair
    row = lax.broadcasted_iota(jnp.int32, (rows, 2 * pair), 0)
    col = lax.broadcasted_iota(jnp.int32, (rows, 2 * pair), 1)
    first = (row % pair) < CHUNK
    band = (first & (col < WINDOW + CHUNK)) | (~first & (col >= CHUNK))
    low = lax.broadcasted_iota(jnp.int32, (pair, LANES), 1) < HEAD_DIM

    def body(c2, carry):
        r0 = pl.multiple_of(c2 * pair, pair)
        valid = band & ((col >= WINDOW) | (i * tq + r0 > 0))
        for h in range(KV_HEADS):
            lt = slice((h // 2) * LANES, (h // 2 + 1) * LANES)
            kg = kbuf[pl.ds(r0, 2 * pair), lt]
            vg = vbuf[pl.ds(r0, 2 * pair), lt]
            qm = jnp.concatenate(
                [q_ref[pl.ds(r0, pair), (h * Q_PER_KV + g) * LANES:(h * Q_PER_KV + g + 1) * LANES]
                 for g in range(Q_PER_KV)], axis=0)
            o = _attend(qm, kg, vg, _sink_column(sink_ref, h, pair), valid)
            _store_heads(o, o_ref, pair, r0, h, low)
        return carry

    lax.fori_loop(0, tq // pair, body, 0)


def _attn_prompt(sinks, q, k, v, bsz, t, tq):
    nt_ = t // tq
    hb = tq // WINDOW
    main = lambda n: pl.BlockSpec((tq, n), lambda b, i: (b * nt_ + i, 0))
    halo = pl.BlockSpec((WINDOW, KV_WIDTH), lambda b, i: (jnp.maximum((b * nt_ + i) * hb - 1, 0), 0))
    return pl.pallas_call(
        _attn_prompt_kernel,
        grid=(bsz, nt_),
        in_specs=[pl.BlockSpec(memory_space=pltpu.SMEM), main(QX_WIDTH), main(KV_WIDTH), halo, main(KV_WIDTH), halo],
        out_specs=main(ATTN_WIDTH),
        out_shape=jax.ShapeDtypeStruct((bsz * t, ATTN_WIDTH), BF16),
        scratch_shapes=[pltpu.VMEM((tq + WINDOW, KV_WIDTH), BF16), pltpu.VMEM((tq + WINDOW, KV_WIDTH), BF16)],
        compiler_params=_cparams(("arbitrary", "arbitrary"), 32),
        name="swa_prompt",
    )(sinks, q, k, k, v, v)


def _attn_sample_kernel(sink_ref, q_ref, k_ref, ck_ref, v_ref, cv_ref, o_ref, kbuf, vbuf):
    ts = q_ref.shape[0]
    keep = ck_ref.shape[0]
    kbuf[0:keep, :] = ck_ref[...].astype(BF16)
    kbuf[keep:, :] = k_ref[...].astype(BF16)
    vbuf[0:keep, :] = cv_ref[...].astype(BF16)
    vbuf[keep:, :] = v_ref[...].astype(BF16)
    low = lax.broadcasted_iota(jnp.int32, (ts, LANES), 1) < HEAD_DIM
    for h in range(KV_HEADS):
        lt = slice((h // 2) * LANES, (h // 2 + 1) * LANES)
        qm = jnp.concatenate(
            [q_ref[:, (h * Q_PER_KV + g) * LANES:(h * Q_PER_KV + g + 1) * LANES] for g in range(Q_PER_KV)], axis=0)
        o = _attend(qm, kbuf[:, lt], vbuf[:, lt], _sink_column(sink_ref, h, ts), None)
        _store_heads(o, o_ref, ts, 0, h, low)


def _attn_sample(sinks, q, k, v, cache_k, cache_v, bsz, ts):
    keep = cache_k.shape[1]
    new = lambda n: pl.BlockSpec((ts, n), lambda b: (b, 0))
    old = pl.BlockSpec((None, keep, KV_WIDTH), lambda b: (b, 0, 0))
    return pl.pallas_call(
        _attn_sample_kernel,
        grid=(bsz,),
        in_specs=[pl.BlockSpec(memory_space=pltpu.SMEM), new(QX_WIDTH), new(KV_WIDTH), old, new(KV_WIDTH), old],
        out_specs=new(ATTN_WIDTH),
        out_shape=jax.ShapeDtypeStruct((bsz * ts, ATTN_WIDTH), BF16),
        scratch_shapes=[pltpu.VMEM((keep + ts, KV_WIDTH), BF16), pltpu.VMEM((keep + ts, KV_WIDTH), BF16)],
        compiler_params=_cparams(("arbitrary",), 32),
        name="swa_sample",
    )(sinks, q, k, cache_k, v, cache_v)


def _ssm_kernel(lv, xbc_ref, z_ref, dt_ref, ctx_ref, h0_ref, cw_ref, cb_ref, alog_ref, dtb_ref, dx_ref, ng_ref,
                selall_ref, e64_ref, e64t_ref, y_ref, hout_ref, tail_ref, xpad):
    L = SSD_CHUNK
    c = pl.program_id(1)

    @pl.when(c == 0)
    def _():
        hout_ref[...] = h0_ref[...]
        xpad[0:SUBLANES, :] = ctx_ref[...]

    xpad[SUBLANES:SUBLANES + lv, :] = xbc_ref[...]
    if lv < L:
        xpad[SUBLANES + lv:, :] = jnp.zeros((L - lv, SSM_CONV_DIM), F32)
    base = SUBLANES - (SSM_CONV - 1)
    acc = cb_ref[...] + cw_ref[0:1, :] * xpad[base:base + L, :]
    for kk in range(1, SSM_CONV):
        acc = acc + cw_ref[kk:kk + 1, :] * xpad[base + kk:base + kk + L, :]
    xc = _silu(acc)
    tail = xbc_ref[lv - SUBLANES:lv, :]
    xpad[0:SUBLANES, :] = tail
    tail_ref[...] = tail

    xs = xc[:, :SSM_INNER]
    bm = xc[:, SSM_INNER:SSM_INNER + SSM_GN].astype(BF16)
    cm = xc[:, SSM_INNER + SSM_GN:].astype(BF16)
    dt_raw = dt_ref[...]
    if lv < L:
        dt_raw = jnp.concatenate([dt_raw, jnp.zeros((L - lv, LANES), F32)], axis=0)
    dt = jnp.logaddexp(dt_raw + dtb_ref[...], 0.0)
    if lv < L:
        dt = jnp.where(lax.broadcasted_iota(jnp.int32, (L, LANES), 0) < lv, dt, 0.0)
    da = dt * (-jnp.exp(alog_ref[...]))

    r_i = lax.broadcasted_iota(jnp.int32, (L, L), 0)
    c_i = lax.broadcasted_iota(jnp.int32, (L, L), 1)
    tri = r_i >= c_i
    tri_b = jnp.where(tri, 1.0, 0.0).astype(BF16)
    eye_b = jnp.where(r_i == c_i, 1.0, 0.0).astype(BF16)
    low = c_i < SSM_HEAD_DIM

    def sel(fn, v):
        hi, mid = _split2(v)
        return fn(hi) + fn(mid)

    acum = sel(lambda p: _dot(tri_b, p), da)
    acum_t = sel(lambda p: _nt(eye_b[:SSM_HEADS], p), acum)
    colmat = sel(lambda p: _dot(p, selall_ref[...]), acum)
    a_last = acum[L - 1:L, :]
    e64 = e64_ref[...]
    dtx = sel(lambda p: _dot(p, e64), dt)
    dtex = sel(lambda p: _dot(p, e64), dt * jnp.exp(a_last - acum))
    eax = sel(lambda p: _dot(p, e64), jnp.exp(acum))
    cdm = sel(lambda p: _nt(e64t_ref[...], p), jnp.broadcast_to(jnp.exp(a_last), (LANES, LANES)))
    xdt = (xs * dtx).astype(BF16)
    xw = (xs * dtex).astype(BF16)

    ys = []
    for g in range(SSM_GROUPS):
        bg = bm[:, g * SSM_STATE:(g + 1) * SSM_STATE]
        cg = cm[:, g * SSM_STATE:(g + 1) * SSM_STATE]
        cbg = _nt(cg, bg)
        for jp in range(2):
            j = 2 * g + jp
            lt = slice(j * LANES, (j + 1) * LANES)
            yp = []
            for hh in range(2):
                h = 2 * j + hh
                seg = colmat[:, h * L:(h + 1) * L] - acum_t[h:h + 1, :]
                dec = jnp.exp(jnp.where(tri, seg, -jnp.inf))
                yp.append(_dot((cbg * dec).astype(BF16), xdt[:, lt]))
            st = hout_ref[j]
            y_off = _nt(cg, st.astype(BF16)) * eax[:, lt]
            xw_t = _nt(eye_b, xw[:, lt]).astype(BF16)
            hout_ref[j] = st * cdm[lt, :] + _dot(xw_t, bg)
            ys.append(jnp.where(low, yp[0], yp[1]) + y_off)
    y = jnp.concatenate(ys, axis=1) + dx_ref[...] * xs
    u = y[:lv] * _silu(z_ref[...])
    gw = SSM_INNER // SSM_GROUPS
    for g in range(SSM_GROUPS):
        ug = u[:, g * gw:(g + 1) * gw]
        ug = ug * lax.rsqrt(jnp.mean(ug * ug, axis=-1, keepdims=True) + EPS)
        y_ref[:, g * gw:(g + 1) * gw] = (ug * ng_ref[:, g * gw:(g + 1) * gw]).astype(y_ref.dtype)


def _ssm(xbc, z, dt, ctx8, h0, prm, bsz, t, lv):
    nc = t // lv
    row = lambda n: pl.BlockSpec((lv, n), lambda b, c: (b * nc + c, 0))
    per_b = lambda shp: pl.BlockSpec((None,) + shp, lambda b, c: (b,) + (0,) * len(shp))
    consts = [prm["conv_w8"], prm["conv_b"], prm["a_log"], prm["dt_bias"], prm["d_x"], prm["norm_g"],
              prm["selall"], prm["e64"], prm["e64t"]]
    state = (SSM_HEADS // 2, 2 * SSM_HEAD_DIM, SSM_STATE)
    return pl.pallas_call(
        functools.partial(_ssm_kernel, lv),
        grid=(bsz, nc),
        in_specs=[row(SSM_CONV_DIM), row(SSM_INNER), row(LANES), per_b((SUBLANES, SSM_CONV_DIM)), per_b(state)]
                 + [_resident(a.shape) for a in consts],
        out_specs=[row(SSM_INNER), per_b(state), per_b((SUBLANES, SSM_CONV_DIM))],
        out_shape=[jax.ShapeDtypeStruct((bsz * t, SSM_INNER), BF16),
                   jax.ShapeDtypeStruct((bsz,) + state, F32),
                   jax.ShapeDtypeStruct((bsz, SUBLANES, SSM_CONV_DIM), F32)],
        scratch_shapes=[pltpu.VMEM((SUBLANES + SSD_CHUNK, SSM_CONV_DIM), F32)],
        compiler_params=_cparams(("arbitrary", "arbitrary"), 40),
        name="ssd_mixer",
    )(xbc, z, dt, ctx8, h0.reshape((bsz,) + state), *consts)


def _proj_resid_kernel(n_in, has_bias, *refs):
    a_refs = refs[:n_in]
    w_ref = refs[n_in]
    pos = n_in + 1
    b_ref = refs[pos] if has_bias else None
    pos += int(has_bias)
    x_ref, gt_ref, o_ref = refs[pos], refs[pos + 1], refs[pos + 2]
    k0 = 0
    acc = None
    for a_ref in a_refs:
        kk = a_ref.shape[1]
        part = _dot(a_ref[...], w_ref[k0:k0 + kk, :])
        acc = part if acc is None else acc + part
        k0 += kk
    if has_bias:
        acc = acc + b_ref[...]
    o_ref[...] = x_ref[...] + gt_ref[...] * acc


def _proj_resid(a_list, w, bias, x, mod, gate_chunk, blocks_per_group, tm, name):
    m = x.shape[0]
    row = lambda n: pl.BlockSpec((tm, n), lambda i: (i, 0))
    has_bias = bias is not None
    args = list(a_list) + [w] + ([bias] if has_bias else []) + [x, mod]
    specs = ([row(a.shape[1]) for a in a_list] + [_resident(w.shape)] + ([_resident(bias.shape)] if has_bias else [])
             + [row(D_MODEL), _mod_spec(mod, gate_chunk, blocks_per_group)])
    return pl.pallas_call(
        functools.partial(_proj_resid_kernel, len(a_list), has_bias),
        grid=(m // tm,),
        in_specs=specs,
        out_specs=row(D_MODEL),
        out_shape=jax.ShapeDtypeStruct((m, D_MODEL), F32),
        compiler_params=_cparams(("arbitrary",), 48),
        name=name,
    )(*args)


def _mlp_kernel(final_norm, x_ref, g_ref, sh_ref, sc_ref, gt_ref, wu_ref, wd_ref, gf_ref, o_ref, h_ref, acc_ref):
    f = pl.program_id(1)

    @pl.when(f == 0)
    def _():
        h_ref[...] = _norm_mod(x_ref[...], g_ref[...], sc_ref[...], sh_ref[...]).astype(BF16)
        acc_ref[...] = jnp.zeros_like(acc_ref)

    u = jnp.square(jnp.maximum(_dot(h_ref[...], wu_ref[...]), 0.0)).astype(BF16)
    acc_ref[...] += _dot(u, wd_ref[...])

    @pl.when(f == pl.num_programs(1) - 1)
    def _():
        y = x_ref[...] + gt_ref[...] * acc_ref[...]
        if final_norm:
            y = y * lax.rsqrt(jnp.mean(y * y, axis=-1, keepdims=True) + EPS) * gf_ref[...]
        o_ref[...] = y


def _mlp(x, g, mod, blocks_per_group, w_up, w_down, g_final, final_norm, tm, tf):
    m = x.shape[0]
    row = pl.BlockSpec((tm, D_MODEL), lambda i, f: (i, 0))
    return pl.pallas_call(
        functools.partial(_mlp_kernel, final_norm),
        grid=(m // tm, D_FF // tf),
        in_specs=[row, _resident((1, D_MODEL)), _mod_spec(mod, 3, blocks_per_group), _mod_spec(mod, 4, blocks_per_group),
                  _mod_spec(mod, 5, blocks_per_group),
                  pl.BlockSpec((D_MODEL, tf), lambda i, f: (0, f)), pl.BlockSpec((tf, D_MODEL), lambda i, f: (f, 0)),
                  _resident((1, D_MODEL))],
        out_specs=row,
        out_shape=jax.ShapeDtypeStruct((m, D_MODEL), F32),
        scratch_shapes=[pltpu.VMEM((tm, D_MODEL), BF16), pltpu.VMEM((tm, D_MODEL), F32)],
        compiler_params=_cparams(("arbitrary", "arbitrary"), 56),
        name="relu2_mlp",
    )(x, g, mod, mod, mod, w_up, w_down, g_final)


def _conf_glu_kernel(x_ref, g_ref, sh_ref, sc_ref, w_ref, b_ref, o_ref):
    a = _norm_mod(x_ref[...], g_ref[...], sc_ref[...], sh_ref[...]).astype(BF16)
    u1 = _dot(a, w_ref[:, :D_MODEL]) + b_ref[:, :D_MODEL]
    u2 = _dot(a, w_ref[:, D_MODEL:]) + b_ref[:, D_MODEL:]
    o_ref[...] = u1 * jax.nn.sigmoid(u2)


def _conf_glu(x, g, mod, blocks_per_group, w1, b1, tm):
    m = x.shape[0]
    row = pl.BlockSpec((tm, D_MODEL), lambda i: (i, 0))
    return pl.pallas_call(
        _conf_glu_kernel,
        grid=(m // tm,),
        in_specs=[row, _resident((1, D_MODEL)), _mod_spec(mod, 0, blocks_per_group), _mod_spec(mod, 1, blocks_per_group),
                  _resident(w1.shape), _resident(b1.shape)],
        out_specs=row,
        out_shape=jax.ShapeDtypeStruct((m, D_MODEL), F32),
        compiler_params=_cparams(("arbitrary",), 56),
        name="conf_glu",
    )(x, g, mod, mod, w1, b1)


def _conf_conv_kernel(u_ref, halo_ref, ctx_ref, w_ref, b_ref, lg_ref, lb_ref, o_ref, xpad, ybuf):
    i = pl.program_id(1)
    tm = u_ref.shape[0]

    @pl.when(i == 0)
    def _():
        xpad[0:CONF_HALO, :] = ctx_ref[...]

    @pl.when(i > 0)
    def _():
        xpad[0:CONF_HALO, :] = halo_ref[...]

    xpad[CONF_HALO:, :] = u_ref[...]
    base = CONF_HALO - (CONF_KERNEL - 1)
    rt = min(tm, 64)
    for ct in range(D_MODEL // LANES):
        lt = slice(ct * LANES, (ct + 1) * LANES)
        for r0 in range(0, tm, rt):
            acc = jnp.broadcast_to(b_ref[:, lt], (rt, LANES))
            for kk in range(CONF_KERNEL):
                acc = acc + w_ref[kk:kk + 1, lt] * xpad[base + r0 + kk:base + r0 + kk + rt, lt]
            ybuf[r0:r0 + rt, lt] = acc
    y = ybuf[...]
    mu = jnp.mean(y, axis=-1, keepdims=True)
    yc = y - mu
    var = jnp.mean(yc * yc, axis=-1, keepdims=True)
    o_ref[...] = _silu(yc * lax.rsqrt(var + EPS) * lg_ref[...] + lb_ref[...]).astype(o_ref.dtype)


def _conf_conv(u, ctx32, w32, b, lg, lb, bsz, t, tm):
    nt_ = t // tm
    rows = u.shape[0]
    row = pl.BlockSpec((tm, D_MODEL), lambda bb, i: (bb * nt_ + i, 0))
    halo = pl.BlockSpec((CONF_HALO, D_MODEL),
                        lambda bb, i: (jnp.clip(((bb * nt_ + i) * tm) // CONF_HALO - 1, 0, rows // CONF_HALO - 1), 0))
    return pl.pallas_call(
        _conf_conv_kernel,
        grid=(bsz, nt_),
        in_specs=[row, halo, pl.BlockSpec((None, CONF_HALO, D_MODEL), lambda bb, i: (bb, 0, 0)),
                  _resident(w32.shape), _resident(b.shape), _resident(lg.shape), _resident(lb.shape)],
        out_specs=row,
        out_shape=jax.ShapeDtypeStruct((rows, D_MODEL), BF16),
        scratch_shapes=[pltpu.VMEM((CONF_HALO + tm, D_MODEL), F32), pltpu.VMEM((tm, D_MODEL), F32)],
        compiler_params=_cparams(("arbitrary", "arbitrary"), 32),
        name="conf_dwconv_ln",
    )(u, u, ctx32, w32, b, lg, lb)


def _rope_tables(pos):
    half = HEAD_DIM // 2
    inv = ROPE_THETA ** (-jnp.arange(half, dtype=F32) / half)
    ang = pos.astype(F32)[:, None] * inv[None, :]
    cos, sin, zero = jnp.cos(ang), jnp.sin(ang), jnp.zeros_like(ang)
    rep = LANES // HEAD_DIM
    tile = lambda a, b: jnp.tile(jnp.concatenate([a, b], axis=1), (1, rep))
    return tile(cos, cos), tile(-sin, zero), tile(zero, sin)


def _trunk(x, bsz, t, mods, blocks_per_group, tabs, tab_blocks, attend, ssm_ctx8, ssm_h0, ssm_lv, conf_ctx32,
           conf_tm, tm, p):
    q, k, v, z, xbc, dt = _inproj(x, p["g_mix"][0], mods[0], blocks_per_group, tabs, tab_blocks, p["w_in"], tm)
    o_attn = attend(q, k, v)
    y_ssm, h_t, tail = _ssm(xbc, z, dt, ssm_ctx8, ssm_h0, p["ssm"], bsz, t, ssm_lv)
    x = _proj_resid([o_attn, y_ssm], p["w_out"], None, x, mods[0], 2, blocks_per_group, tm, "mix_out_proj")
    x = _mlp(x, p["g_mlp"][0], mods[0], blocks_per_group, p["w_up"][0], p["w_down"][0], p["g_final"], False, tm, 1024)
    glu = _conf_glu(x, p["g_mix"][1], mods[1], blocks_per_group, p["conf_w1"], p["conf_b1"], tm)
    cv = _conf_conv(glu, conf_ctx32, p["conf_dw_w32"], p["conf_dw_b"], p["conf_ln_g"], p["conf_ln_b"], bsz, t, conf_tm)
    x = _proj_resid([cv], p["conf_w2"], p["conf_b2"], x, mods[1], 2, blocks_per_group, tm, "conf_out_proj")
    x = _mlp(x, p["g_mlp"][1], mods[1], blocks_per_group, p["w_up"][1], p["w_down"][1], p["g_final"], True, tm, 1024)
    return x, k, v, h_t, tail, glu


def kernel(x_prompt, x_sample, cache_swa_k, cache_swa_v, state_ssm, state_ssm_conv, state_conf_conv, c_prompt, c_sample, w_mod, b_mod, g_mix, g_mlp, w_in, w_out, attn_sinks, ssm_a_log, ssm_dt_bias, ssm_d, ssm_conv_w, ssm_conv_b, ssm_norm_g, conf_w1, conf_b1, conf_dw_w, conf_dw_b, conf_ln_g, conf_ln_b, conf_w2, conf_b2, mlp_w_up, mlp_w_down, g_final):
    bp, tp, d = x_prompt.shape
    bs, ts, _ = x_sample.shape
    depth = w_mod.shape[0]
    row = lambda a: a.reshape(1, -1)
    lane_pad = lambda a: jnp.pad(a.reshape(1, -1), ((0, 0), (0, LANES - a.size)))

    n_seq = bp + bs
    c_all = jnp.pad(jnp.concatenate([c_prompt, c_sample], axis=0), ((0, -n_seq % SUBLANES), (0, 0)))
    mod = _modulation(c_all, w_mod, b_mod)
    mods_p = [mod[l, :bp].reshape(bp, 1, 6 * d) for l in range(depth)]
    mods_s = [jnp.repeat(mod[l, bp:n_seq], ts, axis=0).reshape(1, bs * ts, 6 * d) for l in range(depth)]

    head_of_col = jnp.arange(SSM_HEADS * SSD_CHUNK) // SSD_CHUNK
    selall = (jnp.arange(LANES)[:, None] == head_of_col[None, :]).astype(BF16)
    head_of_feat = jnp.arange(SSM_INNER) // SSM_HEAD_DIM
    e64 = (jnp.arange(LANES)[:, None] == head_of_feat[None, :]).astype(BF16)

    w_in_p = jnp.pad(w_in[0], ((0, 0), (0, IN_PAD - IN_WIDTH))).astype(BF16)
    p = dict(
        g_mix=[row(g_mix[l]) for l in range(depth)], g_mlp=[row(g_mlp[l]) for l in range(depth)], g_final=row(g_final),
        w_in=w_in_p, w_out=w_out[0].astype(BF16),
        w_up=[mlp_w_up[l].astype(BF16) for l in range(depth)], w_down=[mlp_w_down[l].astype(BF16) for l in range(depth)],
        conf_w1=conf_w1[0].astype(BF16), conf_b1=row(conf_b1[0]), conf_w2=conf_w2[0].astype(BF16), conf_b2=row(conf_b2[0]),
        conf_dw_w32=jnp.pad(conf_dw_w[0], ((0, CONF_HALO - CONF_KERNEL), (0, 0))), conf_dw_b=row(conf_dw_b[0]),
        conf_ln_g=row(conf_ln_g[0]), conf_ln_b=row(conf_ln_b[0]),
        ssm=dict(conv_w8=jnp.pad(ssm_conv_w[0], ((0, SUBLANES - SSM_CONV), (0, 0))), conv_b=row(ssm_conv_b[0]),
                 a_log=lane_pad(ssm_a_log[0]), dt_bias=lane_pad(ssm_dt_bias[0]),
                 d_x=row(jnp.repeat(ssm_d[0], SSM_HEAD_DIM)), norm_g=row(ssm_norm_g[0]),
                 selall=selall, e64=e64, e64t=e64.T),
    )
    sinks = attn_sinks[0]
    tm = 512

    tq = 512
    attend_p = lambda q, k, v: _attn_prompt(sinks, q, k, v, bp, tp, tq)
    y_p, k_p, v_p, h_p, tail_p, glu_p = _trunk(
        x_prompt.reshape(bp * tp, d), bp, tp, mods_p, tp // tm, _rope_tables(jnp.arange(tp)), tp // tm, attend_p,
        jnp.zeros((bp, SUBLANES, SSM_CONV_DIM), F32), jnp.zeros((bp, SSM_HEADS, SSM_HEAD_DIM, SSM_STATE), F32),
        min(SSD_CHUNK, tp), jnp.zeros((bp, CONF_HALO, d), F32), 128, tm, p)

    keep = cache_swa_k.shape[2]
    ck = cache_swa_k[0].reshape(bs, keep, KV_WIDTH)
    cv = cache_swa_v[0].reshape(bs, keep, KV_WIDTH)
    attend_s = lambda q, k, v: _attn_sample(sinks, q, k, v, ck, cv, bs, ts)
    tabs_s = tuple(jnp.tile(a, (bs, 1)) for a in _rope_tables(PAST_LEN + jnp.arange(ts)))
    ssm_ctx_s = jnp.pad(state_ssm_conv[0], ((0, 0), (SUBLANES - (SSM_CONV - 1), 0), (0, 0)))
    conf_ctx_s = jnp.pad(state_conf_conv[0], ((0, 0), (CONF_HALO - (CONF_KERNEL - 1), 0), (0, 0)))
    y_s, k_s, v_s, h_s, tail_s, glu_s = _trunk(
        x_sample.reshape(bs * ts, d), bs, ts, mods_s, 1, tabs_s, 1, attend_s,
        ssm_ctx_s, state_ssm[0], ts, conf_ctx_s, ts, bs * ts, p)

    kv_p = lambda a: a.reshape(bp, tp, KV_HEADS, HEAD_DIM)[:, tp - WINDOW:][None]
    kv_s = lambda a: a.reshape(bs, ts, KV_HEADS, HEAD_DIM)[None]
    nctx = SSM_CONV - 1
    conf_keep = CONF_KERNEL - 1
    conf_prompt = glu_p.reshape(bp, tp, d)[:, tp - conf_keep:][None]
    conf_sample = jnp.concatenate([state_conf_conv[0], glu_s.reshape(bs, ts, d)], axis=1)[:, ts:][None]
    return (y_p.reshape(bp, tp, d), y_s.reshape(bs, ts, d), kv_p(k_p), kv_p(v_p), kv_s(k_s), kv_s(v_s),
            h_p.reshape((1, bp) + state_ssm.shape[2:]), h_s.reshape(state_ssm.shape), tail_p[:, SUBLANES - nctx:][None], tail_s[:, SUBLANES - nctx:][None],
            conf_prompt, conf_sample)
```

```python
import functools
import math

import jax
import jax.numpy as jnp
from jax import lax
from jax.experimental import pallas as pl
from jax.experimental.pallas import tpu as pltpu

F32 = jnp.float32
BF16 = jnp.bfloat16

D_MODEL = 2048
CHUNK = 64
WINDOW = 128
PAST_LEN = 1024
HEAD_DIM = 64
Q_HEADS = 16
KV_HEADS = 4
Q_PER_KV = Q_HEADS // KV_HEADS
ATTN_WIDTH = Q_HEADS * HEAD_DIM
KV_WIDTH = KV_HEADS * HEAD_DIM
ATTN_SCALE = HEAD_DIM ** -0.5
ROPE_THETA = 10000.0
SSM_HEADS = 16
SSM_HEAD_DIM = 64
SSM_INNER = SSM_HEADS * SSM_HEAD_DIM
SSM_GROUPS = 4
SSM_STATE = 128
SSM_GN = SSM_GROUPS * SSM_STATE
SSM_CONV = 4
SSM_CONV_DIM = SSM_INNER + 2 * SSM_GN
SSD_CHUNK = 128
Q_END = ATTN_WIDTH
K_END = Q_END + KV_WIDTH
V_END = K_END + KV_WIDTH
Z_END = V_END + SSM_INNER
XBC_END = Z_END + SSM_CONV_DIM
IN_WIDTH = XBC_END + SSM_HEADS
CONF_KERNEL = 31
D_FF = 4 * D_MODEL
EPS = 1e-6

LANES = 128
SUBLANES = 8
QX_WIDTH = Q_HEADS * LANES
IN_PAD = XBC_END + LANES
CONF_HALO = 32
MIB = 1024 * 1024


def _cparams(sem, vmem_mib):
    return pltpu.CompilerParams(dimension_semantics=sem, vmem_limit_bytes=vmem_mib * MIB)


def _resident(shape):
    nd = len(shape)
    return pl.BlockSpec(shape, lambda *_: (0,) * nd, pipeline_mode=pl.Buffered(1))


def _silu(x):
    return x * jax.nn.sigmoid(x)


def _norm_mod(x, g, sc, sh):
    y = x * lax.rsqrt(jnp.mean(x * x, axis=-1, keepdims=True) + EPS) * g
    return y * (1.0 + sc) + sh


def _nt(a, b):
    return lax.dot_general(a, b, (((1,), (1,)), ((), ())), preferred_element_type=F32)


def _dot(a, b):
    return jnp.dot(a, b, preferred_element_type=F32)


def _split2(v):
    hi = v.astype(BF16)
    mid = (v - hi.astype(F32)).astype(BF16)
    return hi, mid


def _mod_kernel(c_ref, w_ref, b_ref, o_ref):
    a = _silu(c_ref[...]).astype(BF16)
    o_ref[0] = _dot(a, w_ref[0].astype(BF16)) + b_ref[0]


def _modulation(c_all, w_mod, b_mod):
    depth, d, n = w_mod.shape
    rows = c_all.shape[0]
    tn = 1024
    return pl.pallas_call(
        _mod_kernel,
        grid=(depth, n // tn),
        in_specs=[
            pl.BlockSpec((rows, d), lambda l, j: (0, 0)),
            pl.BlockSpec((1, d, tn), lambda l, j: (l, 0, j)),
            pl.BlockSpec((1, 1, tn), lambda l, j: (l, 0, j)),
        ],
        out_specs=pl.BlockSpec((1, rows, tn), lambda l, j: (l, 0, j)),
        out_shape=jax.ShapeDtypeStruct((depth, rows, n), F32),
        compiler_params=_cparams(("arbitrary", "arbitrary"), 40),
        name="adaln_mod",
    )(c_all, w_mod, b_mod.reshape(depth, 1, n))


def _mod_spec(mod, k, blocks_per_group):
    return pl.BlockSpec((None, mod.shape[1], D_MODEL), lambda i, *_: (i // blocks_per_group, 0, k))


def _inproj_kernel(x_ref, g_ref, sh_ref, sc_ref, cos_ref, sa_ref, sb_ref, w_ref,
                   q_ref, k_ref, v_ref, z_ref, xbc_ref, dt_ref):
    a = _norm_mod(x_ref[...], g_ref[...], sc_ref[...], sh_ref[...]).astype(BF16)
    cos, sa, sb = cos_ref[...], sa_ref[...], sb_ref[...]
    lane = lax.broadcasted_iota(jnp.int32, (a.shape[0], LANES), 1)
    low = lane < HEAD_DIM

    def rope(t):
        return t * cos + pltpu.roll(t, LANES - HEAD_DIM // 2, 1) * sa + pltpu.roll(t, HEAD_DIM // 2, 1) * sb

    q = _dot(a, w_ref[:, :Q_END])
    for j in range(Q_END // LANES):
        t = rope(q[:, j * LANES:(j + 1) * LANES]) * ATTN_SCALE
        kv_half = (j // 2) % 2
        for half in range(2):
            th = jnp.where(low if half == 0 else ~low, t, 0.0)
            if half != kv_half:
                th = pltpu.roll(th, HEAD_DIM, 1)
            hq = 2 * j + half
            q_ref[:, hq * LANES:(hq + 1) * LANES] = th.astype(BF16)
    k = _dot(a, w_ref[:, Q_END:K_END])
    for j in range(KV_WIDTH // LANES):
        k_ref[:, j * LANES:(j + 1) * LANES] = rope(k[:, j * LANES:(j + 1) * LANES])
    v_ref[...] = _dot(a, w_ref[:, K_END:V_END])
    z_ref[...] = _dot(a, w_ref[:, V_END:Z_END])
    xbc_ref[...] = _dot(a, w_ref[:, Z_END:XBC_END])
    dt_ref[...] = _dot(a, w_ref[:, XBC_END:IN_PAD])


def _inproj(x, g, mod, blocks_per_group, tabs, tab_blocks, w, tm):
    m = x.shape[0]
    row = lambda n: pl.BlockSpec((tm, n), lambda i: (i, 0))
    tab = pl.BlockSpec((tm, LANES), lambda i: (i % tab_blocks, 0))
    outs = [(QX_WIDTH, BF16), (KV_WIDTH, F32), (KV_WIDTH, F32), (SSM_INNER, F32), (SSM_CONV_DIM, F32), (LANES, F32)]
    return pl.pallas_call(
        _inproj_kernel,
        grid=(m // tm,),
        in_specs=[row(D_MODEL), _resident((1, D_MODEL)), _mod_spec(mod, 0, blocks_per_group),
                  _mod_spec(mod, 1, blocks_per_group), tab, tab, tab, _resident(w.shape)],
        out_specs=[row(n) for n, _ in outs],
        out_shape=[jax.ShapeDtypeStruct((m, n), dt) for n, dt in outs],
        compiler_params=_cparams(("arbitrary",), 56),
        name="in_proj",
    )(x, g, mod, mod, *tabs, w)


def _attend(qm, kg, vg, sk, valid):
    s = _nt(qm, kg)
    if valid is not None:
        s = jnp.where(valid, s, -jnp.inf)
    m = jnp.maximum(jnp.max(s, axis=1, keepdims=True), sk)
    p = jnp.exp(s - m)
    den = jnp.sum(p, axis=1, keepdims=True) + jnp.exp(sk - m)
    return _dot(p.astype(BF16), vg) / den


def _sink_column(sink_ref, h, rows):
    grp = lax.broadcasted_iota(jnp.int32, (Q_PER_KV * rows, 1), 0) // rows
    sk = jnp.full((Q_PER_KV * rows, 1), sink_ref[h * Q_PER_KV], F32)
    for g in range(1, Q_PER_KV):
        sk = jnp.where(grp == g, sink_ref[h * Q_PER_KV + g], sk)
    return sk


def _store_heads(o, o_ref, rows, r0, h, low):
    for jj in range(2):
        o0 = o[(2 * jj) * rows:(2 * jj + 1) * rows]
        o1 = o[(2 * jj + 1) * rows:(2 * jj + 2) * rows]
        if h % 2 == 1:
            o0 = pltpu.roll(o0, HEAD_DIM, 1)
        else:
            o1 = pltpu.roll(o1, HEAD_DIM, 1)
        j = 2 * h + jj
        o_ref[pl.ds(r0, rows), j * LANES:(j + 1) * LANES] = jnp.where(low, o0, o1).astype(o_ref.dtype)


def _attn_prompt_kernel(sink_ref, q_ref, k_ref, kh_ref, v_ref, vh_ref, o_ref, kbuf, vbuf):
    i = pl.program_id(1)
    tq = q_ref.shape[0]
    pair = 2 * CHUNK
    kbuf[0:WINDOW, :] = kh_ref[...].astype(BF16)
    kbuf[WINDOW:, :] = k_ref[...].astype(BF16)
    vbuf[0:WINDOW, :] = vh_ref[...].astype(BF16)
    vbuf[WINDOW:, :] = v_ref[...].astype(BF16)
    rows = Q_PER_KV * pair
    row = lax.broadcasted_iota(jnp.int32, (rows, 2 * pair), 0)
    col = lax.broadcasted_iota(jnp.int32, (rows, 2 * pair), 1)
    first = (row % pair) < CHUNK
    band = (first & (col < WINDOW + CHUNK)) | (~first & (col >= CHUNK))
    low = lax.broadcasted_iota(jnp.int32, (pair, LANES), 1) < HEAD_DIM

    def body(c2, carry):
        r0 = pl.multiple_of(c2 * pair, pair)
        valid = band & ((col >= WINDOW) | (i * tq + r0 > 0))
        for h in range(KV_HEADS):
            lt = slice((h // 2) * LANES, (h // 2 + 1) * LANES)
            kg = kbuf[pl.ds(r0, 2 * pair), lt]
            vg = vbuf[pl.ds(r0, 2 * pair), lt]
            qm = jnp.concatenate(
                [q_ref[pl.ds(r0, pair), (h * Q_PER_KV + g) * LANES:(h * Q_PER_KV + g + 1) * LANES]
                 for g in range(Q_PER_KV)], axis=0)
            o = _attend(qm, kg, vg, _sink_column(sink_ref, h, pair), valid)
            _store_heads(o, o_ref, pair, r0, h, low)
        return carry

    lax.fori_loop(0, tq // pair, body, 0)


def _attn_prompt(sinks, q, k, v, bsz, t, tq):
    nt_ = t // tq
    hb = tq // WINDOW
    main = lambda n: pl.BlockSpec((tq, n), lambda b, i: (b * nt_ + i, 0))
    halo = pl.BlockSpec((WINDOW, KV_WIDTH), lambda b, i: (jnp.maximum((b * nt_ + i) * hb - 1, 0), 0))
    return pl.pallas_call(
        _attn_prompt_kernel,
        grid=(bsz, nt_),
        in_specs=[pl.BlockSpec(memory_space=pltpu.SMEM), main(QX_WIDTH), main(KV_WIDTH), halo, main(KV_WIDTH), halo],
        out_specs=main(ATTN_WIDTH),
        out_shape=jax.ShapeDtypeStruct((bsz * t, ATTN_WIDTH), BF16),
        scratch_shapes=[pltpu.VMEM((tq + WINDOW, KV_WIDTH), BF16), pltpu.VMEM((tq + WINDOW, KV_WIDTH), BF16)],
        compiler_params=_cparams(("arbitrary", "arbitrary"), 32),
        name="swa_prompt",
    )(sinks, q, k, k, v, v)


def _attn_sample_kernel(sink_ref, q_ref, k_ref, ck_ref, v_ref, cv_ref, o_ref, kbuf, vbuf):
    ts = q_ref.shape[0]
    keep = ck_ref.shape[0]
    kbuf[0:keep, :] = ck_ref[...].astype(BF16)
    kbuf[keep:, :] = k_ref[...].astype(BF16)
    vbuf[0:keep, :] = cv_ref[...].astype(BF16)
    vbuf[keep:, :] = v_ref[...].astype(BF16)
    low = lax.broadcasted_iota(jnp.int32, (ts, LANES), 1) < HEAD_DIM
    for h in range(KV_HEADS):
        lt = slice((h // 2) * LANES, (h // 2 + 1) * LANES)
        qm = jnp.concatenate(
            [q_ref[:, (h * Q_PER_KV + g) * LANES:(h * Q_PER_KV + g + 1) * LANES] for g in range(Q_PER_KV)], axis=0)
        o = _attend(qm, kbuf[:, lt], vbuf[:, lt], _sink_column(sink_ref, h, ts), None)
        _store_heads(o, o_ref, ts, 0, h, low)


def _attn_sample(sinks, q, k, v, cache_k, cache_v, bsz, ts):
    keep = cache_k.shape[1]
    new = lambda n: pl.BlockSpec((ts, n), lambda b: (b, 0))
    old = pl.BlockSpec((None, keep, KV_WIDTH), lambda b: (b, 0, 0))
    return pl.pallas_call(
        _attn_sample_kernel,
        grid=(bsz,),
        in_specs=[pl.BlockSpec(memory_space=pltpu.SMEM), new(QX_WIDTH), new(KV_WIDTH), old, new(KV_WIDTH), old],
        out_specs=new(ATTN_WIDTH),
        out_shape=jax.ShapeDtypeStruct((bsz * ts, ATTN_WIDTH), BF16),
        scratch_shapes=[pltpu.VMEM((keep + ts, KV_WIDTH), BF16), pltpu.VMEM((keep + ts, KV_WIDTH), BF16)],
        compiler_params=_cparams(("arbitrary",), 32),
        name="swa_sample",
    )(sinks, q, k, cache_k, v, cache_v)


def _ssm_kernel(lv, xbc_ref, z_ref, dt_ref, ctx_ref, h0_ref, cw_ref, cb_ref, alog_ref, dtb_ref, dx_ref, ng_ref,
                selall_ref, e64_ref, e64t_ref, y_ref, hout_ref, tail_ref, xpad, xcb):
    L = SSD_CHUNK
    c = pl.program_id(1)
    tiles = [slice(ct * LANES, (ct + 1) * LANES) for ct in range(SSM_CONV_DIM // LANES)]

    @pl.when(c == 0)
    def _():
        hout_ref[...] = h0_ref[...]
        for ct, lt in enumerate(tiles):
            xpad[ct, 0:SUBLANES, :] = ctx_ref[:, lt]

    for ct, lt in enumerate(tiles):
        xpad[ct, SUBLANES:SUBLANES + lv, :] = xbc_ref[:, lt]
        if lv < L:
            xpad[ct, SUBLANES + lv:, :] = jnp.zeros((L - lv, LANES), F32)
    base = SUBLANES - (SSM_CONV - 1)
    seg = 4
    sub = SUBLANES * seg
    for ct, lt in enumerate(tiles):
        for r0 in range(0, L, sub):
            accs = [jnp.broadcast_to(cb_ref[:, lt], (SUBLANES, LANES))] * seg
            for j in range(seg - 1 + SSM_CONV):
                xj = xpad[ct, pl.ds(base + r0 + j, SUBLANES, stride=seg), :]
                for m in range(seg):
                    if 0 <= j - m < SSM_CONV:
                        accs[m] = accs[m] + cw_ref[j - m:j - m + 1, lt] * xj
            for m in range(seg):
                xcb[ct, pl.ds(r0 + m, SUBLANES, stride=seg), :] = _silu(accs[m])
    for ct, lt in enumerate(tiles):
        xpad[ct, 0:SUBLANES, :] = xbc_ref[lv - SUBLANES:lv, lt]
    tail_ref[...] = xbc_ref[lv - SUBLANES:lv, :]

    n_x = SSM_INNER // LANES
    dt_raw = dt_ref[...]
    if lv < L:
        dt_raw = jnp.concatenate([dt_raw, jnp.zeros((L - lv, LANES), F32)], axis=0)
    dt = jnp.logaddexp(dt_raw + dtb_ref[...], 0.0)
    if lv < L:
        dt = jnp.where(lax.broadcasted_iota(jnp.int32, (L, LANES), 0) < lv, dt, 0.0)
    da = dt * (-jnp.exp(alog_ref[...]))

    r_i = lax.broadcasted_iota(jnp.int32, (L, L), 0)
    c_i = lax.broadcasted_iota(jnp.int32, (L, L), 1)
    tri = r_i >= c_i
    tri_b = jnp.where(tri, 1.0, 0.0).astype(BF16)
    eye_b = jnp.where(r_i == c_i, 1.0, 0.0).astype(BF16)
    low = c_i < SSM_HEAD_DIM

    def sel(fn, v):
        hi, mid = _split2(v)
        return fn(hi) + fn(mid)

    acum = sel(lambda p: _dot(tri_b, p), da)
    acum_t = sel(lambda p: _nt(eye_b[:SSM_HEADS], p), acum)
    colmat = sel(lambda p: _dot(p, selall_ref[...]), acum)
    a_last = acum[L - 1:L, :]
    e64 = e64_ref[...]
    dtx = sel(lambda p: _dot(p, e64), dt)
    dtex = sel(lambda p: _dot(p, e64), dt * jnp.exp(a_last - acum))
    eax = sel(lambda p: _dot(p, e64), jnp.exp(acum))
    cdm = sel(lambda p: _nt(e64t_ref[...], p), jnp.broadcast_to(jnp.exp(a_last), (LANES, LANES)))

    ys = []
    for g in range(SSM_GROUPS):
        bg = xcb[n_x + g].astype(BF16)
        cg = xcb[n_x + SSM_GROUPS + g].astype(BF16)
        cbg = _nt(cg, bg)
        for jp in range(2):
            j = 2 * g + jp
            lt = slice(j * LANES, (j + 1) * LANES)
            xs = xcb[j]
            xdt = (xs * dtx[:, lt]).astype(BF16)
            xw = (xs * dtex[:, lt]).astype(BF16)
            yp = []
            for hh in range(2):
                h = 2 * j + hh
                seg = colmat[:, h * L:(h + 1) * L] - acum_t[h:h + 1, :]
                dec = jnp.exp(jnp.where(tri, seg, -jnp.inf))
                yp.append(_dot((cbg * dec).astype(BF16), xdt))
            st = hout_ref[j]
            y_off = _nt(cg, st.astype(BF16)) * eax[:, lt]
            xw_t = _nt(eye_b, xw).astype(BF16)
            hout_ref[j] = st * cdm[lt, :] + _dot(xw_t, bg)
            ys.append(jnp.where(low, yp[0], yp[1]) + y_off + dx_ref[:, lt] * xs)
    y = jnp.concatenate(ys, axis=1)
    u = y[:lv] * _silu(z_ref[...])
    gw = SSM_INNER // SSM_GROUPS
    for g in range(SSM_GROUPS):
        ug = u[:, g * gw:(g + 1) * gw]
        ug = ug * lax.rsqrt(jnp.mean(ug * ug, axis=-1, keepdims=True) + EPS)
        y_ref[:, g * gw:(g + 1) * gw] = (ug * ng_ref[:, g * gw:(g + 1) * gw]).astype(y_ref.dtype)


def _ssm(xbc, z, dt, ctx8, h0, prm, bsz, t, lv):
    nc = t // lv
    row = lambda n: pl.BlockSpec((lv, n), lambda b, c: (b * nc + c, 0))
    per_b = lambda shp: pl.BlockSpec((None,) + shp, lambda b, c: (b,) + (0,) * len(shp))
    consts = [prm["conv_w8"], prm["conv_b"], prm["a_log"], prm["dt_bias"], prm["d_x"], prm["norm_g"],
              prm["selall"], prm["e64"], prm["e64t"]]
    state = (SSM_HEADS // 2, 2 * SSM_HEAD_DIM, SSM_STATE)
    return pl.pallas_call(
        functools.partial(_ssm_kernel, lv),
        grid=(bsz, nc),
        in_specs=[row(SSM_CONV_DIM), row(SSM_INNER), row(LANES), per_b((SUBLANES, SSM_CONV_DIM)), per_b(state)]
                 + [_resident(a.shape) for a in consts],
        out_specs=[row(SSM_INNER), per_b(state), per_b((SUBLANES, SSM_CONV_DIM))],
        out_shape=[jax.ShapeDtypeStruct((bsz * t, SSM_INNER), BF16),
                   jax.ShapeDtypeStruct((bsz,) + state, F32),
                   jax.ShapeDtypeStruct((bsz, SUBLANES, SSM_CONV_DIM), F32)],
        scratch_shapes=[pltpu.VMEM((SSM_CONV_DIM // LANES, SUBLANES + SSD_CHUNK, LANES), F32),
                        pltpu.VMEM((SSM_CONV_DIM // LANES, SSD_CHUNK, LANES), F32)],
        compiler_params=_cparams(("arbitrary", "arbitrary"), 40),
        name="ssd_mixer",
    )(xbc, z, dt, ctx8, h0.reshape((bsz,) + state), *consts)


def _proj_resid_kernel(n_in, has_bias, *refs):
    a_refs = refs[:n_in]
    w_ref = refs[n_in]
    pos = n_in + 1
    b_ref = refs[pos] if has_bias else None
    pos += int(has_bias)
    x_ref, gt_ref, o_ref = refs[pos], refs[pos + 1], refs[pos + 2]
    k0 = 0
    acc = None
    for a_ref in a_refs:
        kk = a_ref.shape[1]
        part = _dot(a_ref[...], w_ref[k0:k0 + kk, :])
        acc = part if acc is None else acc + part
        k0 += kk
    if has_bias:
        acc = acc + b_ref[...]
    o_ref[...] = x_ref[...] + gt_ref[...] * acc


def _proj_resid(a_list, w, bias, x, mod, gate_chunk, blocks_per_group, tm, name):
    m = x.shape[0]
    row = lambda n: pl.BlockSpec((tm, n), lambda i: (i, 0))
    has_bias = bias is not None
    args = list(a_list) + [w] + ([bias] if has_bias else []) + [x, mod]
    specs = ([row(a.shape[1]) for a in a_list] + [_resident(w.shape)] + ([_resident(bias.shape)] if has_bias else [])
             + [row(D_MODEL), _mod_spec(mod, gate_chunk, blocks_per_group)])
    return pl.pallas_call(
        functools.partial(_proj_resid_kernel, len(a_list), has_bias),
        grid=(m // tm,),
        in_specs=specs,
        out_specs=row(D_MODEL),
        out_shape=jax.ShapeDtypeStruct((m, D_MODEL), F32),
        compiler_params=_cparams(("arbitrary",), 48),
        name=name,
    )(*args)


def _mlp_kernel(final_norm, x_ref, xn_ref, g_ref, sh_ref, sc_ref, shn_ref, scn_ref, gt_ref, wu_ref, wd_ref, gf_ref,
                o_ref, h_ref, hn_ref, acc_ref):
    i = pl.program_id(0)
    f = pl.program_id(1)
    rs = xn_ref.shape[0]

    @pl.when((i == 0) & (f == 0))
    def _():
        h_ref[...] = _norm_mod(x_ref[...], g_ref[...], sc_ref[...], sh_ref[...]).astype(BF16)

    @pl.when((i > 0) & (f == 0))
    def _():
        h_ref[...] = hn_ref[...]

    @pl.when(f == 0)
    def _():
        acc_ref[...] = jnp.zeros_like(acc_ref)

    u = jnp.square(jnp.maximum(_dot(h_ref[...], wu_ref[...]), 0.0)).astype(BF16)
    acc_ref[...] += _dot(u, wd_ref[...])

    hn = _norm_mod(xn_ref[...], g_ref[...], scn_ref[...], shn_ref[...]).astype(BF16)
    hn_ref[pl.ds(pl.multiple_of(f * rs, rs), rs), :] = hn

    @pl.when(f == pl.num_programs(1) - 1)
    def _():
        y = x_ref[...] + gt_ref[...] * acc_ref[...]
        if final_norm:
            y = y * lax.rsqrt(jnp.mean(y * y, axis=-1, keepdims=True) + EPS) * gf_ref[...]
        o_ref[...] = y


def _mlp(x, g, mod, blocks_per_group, w_up, w_down, g_final, final_norm, tm, tf):
    m = x.shape[0]
    nblk, nf = m // tm, D_FF // tf
    rs = tm // nf
    per_row = mod.shape[1] != 1
    assert not per_row or nblk == 1
    row = pl.BlockSpec((tm, D_MODEL), lambda i, f: (i, 0))
    nxt = lambda i: jnp.minimum(i + 1, nblk - 1)
    x_next = pl.BlockSpec((rs, D_MODEL), lambda i, f: (nxt(i) * nf + f, 0))

    def mod_next(k):
        if per_row:
            return pl.BlockSpec((None, rs, D_MODEL), lambda i, f: (0, f, k))
        return pl.BlockSpec((None, 1, D_MODEL), lambda i, f: (nxt(i) // blocks_per_group, 0, k))

    return pl.pallas_call(
        functools.partial(_mlp_kernel, final_norm),
        grid=(nblk, nf),
        in_specs=[row, x_next, _resident((1, D_MODEL)), _mod_spec(mod, 3, blocks_per_group),
                  _mod_spec(mod, 4, blocks_per_group), mod_next(3), mod_next(4), _mod_spec(mod, 5, blocks_per_group),
                  pl.BlockSpec((D_MODEL, tf), lambda i, f: (0, f)), pl.BlockSpec((tf, D_MODEL), lambda i, f: (f, 0)),
                  _resident((1, D_MODEL))],
        out_specs=row,
        out_shape=jax.ShapeDtypeStruct((m, D_MODEL), F32),
        scratch_shapes=[pltpu.VMEM((tm, D_MODEL), BF16), pltpu.VMEM((tm, D_MODEL), BF16),
                        pltpu.VMEM((tm, D_MODEL), F32)],
        compiler_params=_cparams(("arbitrary", "arbitrary"), 56),
        name="relu2_mlp",
    )(x, x, g, mod, mod, mod, mod, mod, w_up, w_down, g_final)


def _conf_glu_kernel(x_ref, g_ref, sh_ref, sc_ref, w_ref, b_ref, o_ref):
    a = _norm_mod(x_ref[...], g_ref[...], sc_ref[...], sh_ref[...]).astype(BF16)
    u1 = _dot(a, w_ref[:, :D_MODEL]) + b_ref[:, :D_MODEL]
    u2 = _dot(a, w_ref[:, D_MODEL:]) + b_ref[:, D_MODEL:]
    o_ref[...] = u1 * jax.nn.sigmoid(u2)


def _conf_glu(x, g, mod, blocks_per_group, w1, b1, tm):
    m = x.shape[0]
    row = pl.BlockSpec((tm, D_MODEL), lambda i: (i, 0))
    return pl.pallas_call(
        _conf_glu_kernel,
        grid=(m // tm,),
        in_specs=[row, _resident((1, D_MODEL)), _mod_spec(mod, 0, blocks_per_group), _mod_spec(mod, 1, blocks_per_group),
                  _resident(w1.shape), _resident(b1.shape)],
        out_specs=row,
        out_shape=jax.ShapeDtypeStruct((m, D_MODEL), F32),
        compiler_params=_cparams(("arbitrary",), 56),
        name="conf_glu",
    )(x, g, mod, mod, w1, b1)


def _conf_conv_kernel(u_ref, halo_ref, ctx_ref, w_ref, b_ref, lg_ref, lb_ref, o_ref, xpad, ybuf):
    i = pl.program_id(1)
    tm = u_ref.shape[0]
    n_ct = D_MODEL // LANES
    tiles = [slice(ct * LANES, (ct + 1) * LANES) for ct in range(n_ct)]

    @pl.when(i == 0)
    def _():
        for ct, lt in enumerate(tiles):
            xpad[ct, 0:CONF_HALO, :] = ctx_ref[:, lt]

    @pl.when(i > 0)
    def _():
        for ct, lt in enumerate(tiles):
            xpad[ct, 0:CONF_HALO, :] = halo_ref[:, lt]

    for ct, lt in enumerate(tiles):
        xpad[ct, CONF_HALO:, :] = u_ref[:, lt]
    base = CONF_HALO - (CONF_KERNEL - 1)
    seg = min(4, tm // SUBLANES)
    sub = SUBLANES * seg
    for ct, lt in enumerate(tiles):
        for r0 in range(0, tm, sub):
            accs = [jnp.broadcast_to(b_ref[:, lt], (SUBLANES, LANES))] * seg
            for j in range(seg - 1 + CONF_KERNEL):
                xj = xpad[ct, pl.ds(base + r0 + j, SUBLANES, stride=seg), :]
                for m in range(seg):
                    if 0 <= j - m < CONF_KERNEL:
                        accs[m] = accs[m] + w_ref[j - m:j - m + 1, lt] * xj
            for m in range(seg):
                ybuf[ct, pl.ds(r0 + m, SUBLANES, stride=seg), :] = accs[m]
    y = jnp.concatenate([ybuf[ct] for ct in range(n_ct)], axis=1)
    mu = jnp.mean(y, axis=-1, keepdims=True)
    yc = y - mu
    var = jnp.mean(yc * yc, axis=-1, keepdims=True)
    o_ref[...] = _silu(yc * lax.rsqrt(var + EPS) * lg_ref[...] + lb_ref[...]).astype(o_ref.dtype)


def _conf_conv(u, ctx32, w32, b, lg, lb, bsz, t, tm):
    nt_ = t // tm
    rows = u.shape[0]
    row = pl.BlockSpec((tm, D_MODEL), lambda bb, i: (bb * nt_ + i, 0))
    halo = pl.BlockSpec((CONF_HALO, D_MODEL),
                        lambda bb, i: (jnp.clip(((bb * nt_ + i) * tm) // CONF_HALO - 1, 0, rows // CONF_HALO - 1), 0))
    return pl.pallas_call(
        _conf_conv_kernel,
        grid=(bsz, nt_),
        in_specs=[row, halo, pl.BlockSpec((None, CONF_HALO, D_MODEL), lambda bb, i: (bb, 0, 0)),
                  _resident(w32.shape), _resident(b.shape), _resident(lg.shape), _resident(lb.shape)],
        out_specs=row,
        out_shape=jax.ShapeDtypeStruct((rows, D_MODEL), BF16),
        scratch_shapes=[pltpu.VMEM((D_MODEL // LANES, CONF_HALO + tm, LANES), F32),
                        pltpu.VMEM((D_MODEL // LANES, tm, LANES), F32)],
        compiler_params=_cparams(("arbitrary", "arbitrary"), 32),
        name="conf_dwconv_ln",
    )(u, u, ctx32, w32, b, lg, lb)


def _rope_tables(pos):
    half = HEAD_DIM // 2
    inv = ROPE_THETA ** (-jnp.arange(half, dtype=F32) / half)
    ang = pos.astype(F32)[:, None] * inv[None, :]
    cos, sin, zero = jnp.cos(ang), jnp.sin(ang), jnp.zeros_like(ang)
    rep = LANES // HEAD_DIM
    tile = lambda a, b: jnp.tile(jnp.concatenate([a, b], axis=1), (1, rep))
    return tile(cos, cos), tile(-sin, zero), tile(zero, sin)


def _trunk(x, bsz, t, mods, blocks_per_group, tabs, tab_blocks, attend, ssm_ctx8, ssm_h0, ssm_lv, conf_ctx32,
           conf_tm, tm, p):
    q, k, v, z, xbc, dt = _inproj(x, p["g_mix"][0], mods[0], blocks_per_group, tabs, tab_blocks, p["w_in"], tm)
    o_attn = attend(q, k, v)
    y_ssm, h_t, tail = _ssm(xbc, z, dt, ssm_ctx8, ssm_h0, p["ssm"], bsz, t, ssm_lv)
    x = _proj_resid([o_attn, y_ssm], p["w_out"], None, x, mods[0], 2, blocks_per_group, tm, "mix_out_proj")
    x = _mlp(x, p["g_mlp"][0], mods[0], blocks_per_group, p["w_up"][0], p["w_down"][0], p["g_final"], False, tm, 1024)
    glu = _conf_glu(x, p["g_mix"][1], mods[1], blocks_per_group, p["conf_w1"], p["conf_b1"], tm)
    cv = _conf_conv(glu, conf_ctx32, p["conf_dw_w32"], p["conf_dw_b"], p["conf_ln_g"], p["conf_ln_b"], bsz, t, conf_tm)
    x = _proj_resid([cv], p["conf_w2"], p["conf_b2"], x, mods[1], 2, blocks_per_group, tm, "conf_out_proj")
    x = _mlp(x, p["g_mlp"][1], mods[1], blocks_per_group, p["w_up"][1], p["w_down"][1], p["g_final"], True, tm, 1024)
    return x, k, v, h_t, tail, glu


def kernel(x_prompt, x_sample, cache_swa_k, cache_swa_v, state_ssm, state_ssm_conv, state_conf_conv, c_prompt, c_sample, w_mod, b_mod, g_mix, g_mlp, w_in, w_out, attn_sinks, ssm_a_log, ssm_dt_bias, ssm_d, ssm_conv_w, ssm_conv_b, ssm_norm_g, conf_w1, conf_b1, conf_dw_w, conf_dw_b, conf_ln_g, conf_ln_b, conf_w2, conf_b2, mlp_w_up, mlp_w_down, g_final):
    bp, tp, d = x_prompt.shape
    bs, ts, _ = x_sample.shape
    depth = w_mod.shape[0]
    row = lambda a: a.reshape(1, -1)
    lane_pad = lambda a: jnp.pad(a.reshape(1, -1), ((0, 0), (0, LANES - a.size)))

    n_seq = bp + bs
    c_all = jnp.pad(jnp.concatenate([c_prompt, c_sample], axis=0), ((0, -n_seq % SUBLANES), (0, 0)))
    mod = _modulation(c_all, w_mod, b_mod)
    mods_p = [mod[l, :bp].reshape(bp, 1, 6 * d) for l in range(depth)]
    mods_s = [jnp.repeat(mod[l, bp:n_seq], ts, axis=0).reshape(1, bs * ts, 6 * d) for l in range(depth)]

    head_of_col = jnp.arange(SSM_HEADS * SSD_CHUNK) // SSD_CHUNK
    selall = (jnp.arange(LANES)[:, None] == head_of_col[None, :]).astype(BF16)
    head_of_feat = jnp.arange(SSM_INNER) // SSM_HEAD_DIM
    e64 = (jnp.arange(LANES)[:, None] == head_of_feat[None, :]).astype(BF16)

    w_in_p = jnp.pad(w_in[0], ((0, 0), (0, IN_PAD - IN_WIDTH))).astype(BF16)
    p = dict(
        g_mix=[row(g_mix[l]) for l in range(depth)], g_mlp=[row(g_mlp[l]) for l in range(depth)], g_final=row(g_final),
        w_in=w_in_p, w_out=w_out[0].astype(BF16),
        w_up=[mlp_w_up[l].astype(BF16) for l in range(depth)], w_down=[mlp_w_down[l].astype(BF16) for l in range(depth)],
        conf_w1=conf_w1[0].astype(BF16), conf_b1=row(conf_b1[0]), conf_w2=conf_w2[0].astype(BF16), conf_b2=row(conf_b2[0]),
        conf_dw_w32=jnp.pad(conf_dw_w[0], ((0, CONF_HALO - CONF_KERNEL), (0, 0))), conf_dw_b=row(conf_dw_b[0]),
        conf_ln_g=row(conf_ln_g[0]), conf_ln_b=row(conf_ln_b[0]),
        ssm=dict(conv_w8=jnp.pad(ssm_conv_w[0], ((0, SUBLANES - SSM_CONV), (0, 0))), conv_b=row(ssm_conv_b[0]),
                 a_log=lane_pad(ssm_a_log[0]), dt_bias=lane_pad(ssm_dt_bias[0]),
                 d_x=row(jnp.repeat(ssm_d[0], SSM_HEAD_DIM)), norm_g=row(ssm_norm_g[0]),
                 selall=selall, e64=e64, e64t=e64.T),
    )
    sinks = attn_sinks[0]
    tm = 512

    tq = 512
    attend_p = lambda q, k, v: _attn_prompt(sinks, q, k, v, bp, tp, tq)
    y_p, k_p, v_p, h_p, tail_p, glu_p = _trunk(
        x_prompt.reshape(bp * tp, d), bp, tp, mods_p, tp // tm, _rope_tables(jnp.arange(tp)), tp // tm, attend_p,
        jnp.zeros((bp, SUBLANES, SSM_CONV_DIM), F32), jnp.zeros((bp, SSM_HEADS, SSM_HEAD_DIM, SSM_STATE), F32),
        min(SSD_CHUNK, tp), jnp.zeros((bp, CONF_HALO, d), F32), 128, tm, p)

    keep = cache_swa_k.shape[2]
    ck = cache_swa_k[0].reshape(bs, keep, KV_WIDTH)
    cv = cache_swa_v[0].reshape(bs, keep, KV_WIDTH)
    attend_s = lambda q, k, v: _attn_sample(sinks, q, k, v, ck, cv, bs, ts)
    tabs_s = tuple(jnp.tile(a, (bs, 1)) for a in _rope_tables(PAST_LEN + jnp.arange(ts)))
    ssm_ctx_s = jnp.pad(state_ssm_conv[0], ((0, 0), (SUBLANES - (SSM_CONV - 1), 0), (0, 0)))
    conf_ctx_s = jnp.pad(state_conf_conv[0], ((0, 0), (CONF_HALO - (CONF_KERNEL - 1), 0), (0, 0)))
    y_s, k_s, v_s, h_s, tail_s, glu_s = _trunk(
        x_sample.reshape(bs * ts, d), bs, ts, mods_s, 1, tabs_s, 1, attend_s,
        ssm_ctx_s, state_ssm[0], ts, conf_ctx_s, ts, bs * ts, p)

    kv_p = lambda a: a.reshape(bp, tp, KV_WIDTH)[:, tp - WINDOW:].reshape(1, bp, WINDOW, KV_HEADS, HEAD_DIM)
    kv_s = lambda a: a.reshape(bs, ts, KV_HEADS, HEAD_DIM)[None]
    nctx = SSM_CONV - 1
    conf_keep = CONF_KERNEL - 1
    conf_prompt = glu_p.reshape(bp, tp, d)[:, tp - conf_keep:][None]
    conf_sample = jnp.concatenate([state_conf_conv[0], glu_s.reshape(bs, ts, d)], axis=1)[:, ts:][None]
    return (y_p.reshape(bp, tp, d), y_s.reshape(bs, ts, d), kv_p(k_p), kv_p(v_p), kv_s(k_s), kv_s(v_s),
            h_p.reshape((1, bp) + state_ssm.shape[2:]), h_s.reshape(state_ssm.shape), tail_p[:, SUBLANES - nctx:][None], tail_s[:, SUBLANES - nctx:][None],
            conf_prompt, conf_sample)
```
